```python
import math
import jax
import jax.numpy as jnp
from jax import lax
import numpy as np

D_MODEL = 1024
BATCH = 8
SEQ = 4096
DEPTH = 1

GRID_W = 64
CTX_LEN = 256
MIX_WIDTH = D_MODEL
ATTN_WIDTH = D_MODEL // 2
HEAD_DIM = 64
N_Q_HEADS = ATTN_WIDTH // HEAD_DIM
N_KV_HEADS = 2
GQA_GROUP = N_Q_HEADS // N_KV_HEADS
WINDOW = 128
BLOCK = 128
CONV_WIDTH_CH = MIX_WIDTH - ATTN_WIDTH
CONV_KERNEL = 31
FFN_HIDDEN = 2816
FFN_CONV = 3
ROPE_BASE = 10000.0
ROPE_AXIS_DIM = HEAD_DIM // 2
NORM_EPS = 1e-6
NEG_INF = -1e30

Q_COLS = N_Q_HEADS * HEAD_DIM
KV_COLS = N_KV_HEADS * HEAD_DIM
GLU_COLS = 2 * CONV_WIDTH_CH
IN_COLS = Q_COLS + 2 * KV_COLS + GLU_COLS

kernel_name = "hybrid_window_gqa_conformer_convffn_dit_layer"


def rms_norm(x, w):
    xf = x.astype(jnp.float32)
    y = xf * lax.rsqrt(jnp.mean(xf * xf, axis=-1, keepdims=True) + NORM_EPS)
    return (y * w.astype(jnp.float32)).astype(x.dtype)


def layer_norm(x, w, b):
    xf = x.astype(jnp.float32)
    mu = jnp.mean(xf, axis=-1, keepdims=True)
    var = jnp.mean(jnp.square(xf - mu), axis=-1, keepdims=True)
    y = (xf - mu) * lax.rsqrt(var + NORM_EPS)
    return (y * w.astype(jnp.float32) + b.astype(jnp.float32)).astype(x.dtype)


def modulate(h, shift, scale):
    return h * (1 + scale) + shift


def depthwise_conv(u, w, b):
    width = w.shape[0]
    pad = (width - 1) // 2
    y = lax.conv_general_dilated(
        u, w[:, None, :].astype(u.dtype), window_strides=(1,),
        padding=[(pad, pad)], dimension_numbers=("NWC", "WIO", "NWC"),
        feature_group_count=u.shape[-1])
    return y + b.astype(u.dtype)


def axial_rope_tables(row, col):
    inv = ROPE_BASE ** (-jnp.arange(0, ROPE_AXIS_DIM, 2, dtype=jnp.float32) / ROPE_AXIS_DIM)
    ang_r = row.astype(jnp.float32)[:, None] * inv
    ang_c = col.astype(jnp.float32)[:, None] * inv
    ang = jnp.concatenate([ang_r, ang_r, ang_c, ang_c], axis=-1)
    return jnp.cos(ang), jnp.sin(ang)


def rotate_half(u):
    u1, u2 = jnp.split(u, 2, axis=-1)
    return jnp.concatenate([-u2, u1], axis=-1)


def apply_axial_rope(x, cos, sin):
    s, hd = cos.shape
    shape = (s,) + (1,) * (x.ndim - 3) + (hd,)
    cos = cos.reshape(shape).astype(x.dtype)
    sin = sin.reshape(shape).astype(x.dtype)
    xr, xc = jnp.split(x, 2, axis=-1)
    rotated = jnp.concatenate([rotate_half(xr), rotate_half(xc)], axis=-1)
    return x * cos + rotated * sin


def windowed_gqa_with_context(q, k, v, k_ctx, v_ctx, sink):
    b, s = q.shape[0], q.shape[1]
    nb = s // BLOCK
    n_ctx = k_ctx.shape[1]
    scale = 1.0 / math.sqrt(HEAD_DIM)
    k_pad = jnp.pad(k, ((0, 0), (BLOCK, BLOCK), (0, 0), (0, 0)))
    v_pad = jnp.pad(v, ((0, 0), (BLOCK, BLOCK), (0, 0), (0, 0)))
    q_blocks = jnp.moveaxis(q.reshape(b, nb, BLOCK, N_KV_HEADS, GQA_GROUP, HEAD_DIM), 1, 0)
    offs_q = jnp.arange(BLOCK)
    offs_k = jnp.arange(3 * BLOCK)
    sink_col = jnp.broadcast_to(sink.astype(jnp.float32)[None, :, :, None, None],
                                (b, N_KV_HEADS, GQA_GROUP, BLOCK, 1))

    def one_block(args):
        i, qb = args
        kb = lax.dynamic_slice_in_dim(k_pad, i * BLOCK, 3 * BLOCK, axis=1)
        vb = lax.dynamic_slice_in_dim(v_pad, i * BLOCK, 3 * BLOCK, axis=1)
        qpos = i * BLOCK + offs_q
        kpos = (i - 1) * BLOCK + offs_k
        valid = ((jnp.abs(qpos[:, None] - kpos[None, :]) <= WINDOW)
                 & (kpos[None, :] >= 0) & (kpos[None, :] < s))
        s_loc = jnp.einsum("bqhgd,bkhd->bhgqk", qb, kb).astype(jnp.float32) * scale
        s_loc = jnp.where(valid, s_loc, NEG_INF)
        s_ctx = jnp.einsum("bqhgd,bkhd->bhgqk", qb, k_ctx).astype(jnp.float32) * scale
        p = jax.nn.softmax(jnp.concatenate([sink_col, s_ctx, s_loc], axis=-1), axis=-1)
        p = p.astype(v.dtype)
        o = (jnp.einsum("bhgqk,bkhd->bqhgd", p[..., 1:1 + n_ctx], v_ctx)
             + jnp.einsum("bhgqk,bkhd->bqhgd", p[..., 1 + n_ctx:], vb))
        return o

    out = lax.map(one_block, (jnp.arange(nb), q_blocks))
    return jnp.moveaxis(out, 0, 1).reshape(b, s, N_KV_HEADS * GQA_GROUP * HEAD_DIM)


def setup_inputs(seed: int = 0) -> dict:
    key = jax.random.key(seed)
    ks = jax.random.split(key, 24)
    f32 = jnp.float32

    def nrm(k, shape, scale):
        return jax.random.normal(k, shape, f32) * scale

    L = DEPTH
    return {
        "x": nrm(ks[0], (BATCH, SEQ, D_MODEL), 1.0),
        "c": nrm(ks[1], (BATCH, D_MODEL), 1.0),
        "ctx": nrm(ks[2], (BATCH, CTX_LEN, D_MODEL), 1.0),
        "c_ctx": nrm(ks[3], (D_MODEL,), 1.0),
        "w_mod": nrm(ks[4], (L, D_MODEL, 6 * D_MODEL), 0.5 * D_MODEL ** -0.5),
        "b_mod": nrm(ks[5], (L, 6 * D_MODEL), 0.02),
        "norm_mix_w": 1.0 + nrm(ks[6], (L, D_MODEL), 0.02),
        "w_in": nrm(ks[7], (L, D_MODEL, IN_COLS), D_MODEL ** -0.5),
        "q_norm_w": 1.0 + nrm(ks[8], (L, HEAD_DIM), 0.02),
        "k_norm_w": 1.0 + nrm(ks[9], (L, HEAD_DIM), 0.02),
        "sink_logit": nrm(ks[10], (L, N_Q_HEADS), 0.5),
        "conv_w": nrm(ks[11], (L, CONV_KERNEL, CONV_WIDTH_CH), CONV_KERNEL ** -0.5),
        "conv_b": nrm(ks[12], (L, CONV_WIDTH_CH), 0.02),
        "conv_norm_w": 1.0 + nrm(ks[13], (L, CONV_WIDTH_CH), 0.02),
        "conv_norm_b": nrm(ks[14], (L, CONV_WIDTH_CH), 0.02),
        "w_out": nrm(ks[15], (L, MIX_WIDTH, D_MODEL), MIX_WIDTH ** -0.5),
        "norm_ffn_w": 1.0 + nrm(ks[16], (L, D_MODEL), 0.02),
        "w_up": nrm(ks[17], (L, D_MODEL, 2 * FFN_HIDDEN), D_MODEL ** -0.5),
        "ffn_conv_w": nrm(ks[18], (L, FFN_CONV, 2 * FFN_HIDDEN), FFN_CONV ** -0.5),
        "ffn_conv_b": nrm(ks[19], (L, 2 * FFN_HIDDEN), 0.02),
        "w_down": nrm(ks[20], (L, FFN_HIDDEN, D_MODEL), FFN_HIDDEN ** -0.5),
    }


def reference(x, c, ctx, c_ctx, w_mod, b_mod, norm_mix_w, w_in, q_norm_w, k_norm_w,
              sink_logit, conv_w, conv_b, conv_norm_w, conv_norm_b, w_out,
              norm_ffn_w, w_up, ffn_conv_w, ffn_conv_b, w_down):
    b, s, d = x.shape
    rows = s // GRID_W
    row = jnp.broadcast_to(jnp.arange(rows)[:, None], (rows, GRID_W)).reshape(s)
    col = jnp.broadcast_to(jnp.arange(GRID_W)[None, :], (rows, GRID_W)).reshape(s)
    cos, sin = axial_rope_tables(row, col)

    for l in range(DEPTH):
        mod = jax.nn.silu(c) @ w_mod[l] + b_mod[l]
        sh1, sc1, g1, sh2, sc2, g2 = [m[:, None, :] for m in jnp.split(mod, 6, axis=-1)]
        mod_ctx = jax.nn.silu(c_ctx) @ w_mod[l] + b_mod[l]
        sh1c, sc1c = mod_ctx[:d], mod_ctx[d:2 * d]

        h = modulate(rms_norm(x, norm_mix_w[l]), sh1, sc1)
        hc = modulate(rms_norm(ctx, norm_mix_w[l]), sh1c, sc1c)
        proj = h @ w_in[l]
        q = proj[..., :Q_COLS].reshape(b, s, N_KV_HEADS, GQA_GROUP, HEAD_DIM)
        k = proj[..., Q_COLS:Q_COLS + KV_COLS].reshape(b, s, N_KV_HEADS, HEAD_DIM)
        v = proj[..., Q_COLS + KV_COLS:Q_COLS + 2 * KV_COLS].reshape(b, s, N_KV_HEADS, HEAD_DIM)
        glu_in = proj[..., Q_COLS + 2 * KV_COLS:]

        kv_ctx = hc @ w_in[l][:, Q_COLS:Q_COLS + 2 * KV_COLS]
        n_ctx = ctx.shape[1]
        k_ctx = rms_norm(kv_ctx[..., :KV_COLS].reshape(b, n_ctx, N_KV_HEADS, HEAD_DIM), k_norm_w[l])
        v_ctx = kv_ctx[..., KV_COLS:].reshape(b, n_ctx, N_KV_HEADS, HEAD_DIM)

        q = apply_axial_rope(rms_norm(q, q_norm_w[l]), cos, sin)
        k = apply_axial_rope(rms_norm(k, k_norm_w[l]), cos, sin)
        sink = sink_logit[l].reshape(N_KV_HEADS, GQA_GROUP)
        attn_out = windowed_gqa_with_context(q, k, v, k_ctx, v_ctx, sink)

        ga, gb = jnp.split(glu_in, 2, axis=-1)
        u = ga * jax.nn.sigmoid(gb)
        u = depthwise_conv(u, conv_w[l], conv_b[l])
        u = jax.nn.silu(layer_norm(u, conv_norm_w[l], conv_norm_b[l]))

        mix = jnp.concatenate([attn_out, u], axis=-1) @ w_out[l]
        x = x + g1 * mix

        h2 = modulate(rms_norm(x, norm_ffn_w[l]), sh2, sc2)
        up = depthwise_conv(h2 @ w_up[l], ffn_conv_w[l], ffn_conv_b[l])
        gate, val = jnp.split(up, 2, axis=-1)
        x = x + g2 * ((jax.nn.silu(gate) * val) @ w_down[l])
    return x
```

```python
import functools
import math

import jax
import jax.numpy as jnp
from jax import lax
from jax.experimental import pallas as pl
from jax.experimental.pallas import tpu as pltpu

F32 = jnp.float32
BF16 = jnp.bfloat16

LANES = 128
GRID_W = 64
HEAD_DIM = 64
N_Q_HEADS = 8
N_KV_HEADS = 2
WINDOW = 128
BLOCK = 128
CONV_KERNEL = 31
CONV_HALO = 16
FFN_HALO = 8
ROPE_BASE = 10000.0
ROPE_AXIS_DIM = HEAD_DIM // 2
NORM_EPS = 1e-6
NEG_INF = -1e30

MOD_ROWS = 16
MOD_COLS = 1536
PROJ_ROWS = 512
MIX_ROWS = 256
CONV_ROWS = 32
FFN_ROWS = 256


def _sigmoid(x):
    return 1.0 / (1.0 + jnp.exp(-x))


def _dot(a, b):
    return jnp.dot(a, b, preferred_element_type=F32)


def _dot_t(a, b):
    return lax.dot_general(a, b, (((1,), (1,)), ((), ())), preferred_element_type=F32)


def _rms_modulate(x, norm_w, shift, scale):
    r = lax.rsqrt(jnp.mean(x * x, axis=-1, keepdims=True) + NORM_EPS)
    return (x * r) * norm_w * (1.0 + scale) + shift


def _lane_lt64():
    return lax.broadcasted_iota(jnp.int32, (1, LANES), 1) < HEAD_DIM


def _dup_heads(x):
    lt = _lane_lt64()
    sw = pltpu.roll(x, HEAD_DIM, axis=1)
    return jnp.where(lt, x, sw), jnp.where(lt, sw, x)


def _mod_kernel(c_ref, w_ref, b_ref, o_ref):
    cc = c_ref[...]
    a = (cc * _sigmoid(cc)).astype(BF16)
    o_ref[...] = _dot(a, w_ref[...].astype(BF16)) + b_ref[...]


def _mod_call(cc, w_mod, b_mod):
    d, n = w_mod.shape
    return pl.pallas_call(
        _mod_kernel,
        grid=(n // MOD_COLS,),
        in_specs=[
            pl.BlockSpec((MOD_ROWS, d), lambda j: (0, 0)),
            pl.BlockSpec((d, MOD_COLS), lambda j: (0, j)),
            pl.BlockSpec((1, MOD_COLS), lambda j: (0, j)),
        ],
        out_specs=pl.BlockSpec((MOD_ROWS, MOD_COLS), lambda j: (0, j)),
        out_shape=jax.ShapeDtypeStruct((MOD_ROWS, n), F32),
        name="mod",
    )(cc, w_mod, b_mod)


def _ctx_kernel(x_ref, mod_ref, nw_ref, w_ref, knw_ref, seg_ref, k_ref, v_ref):
    h = _rms_modulate(x_ref[...], nw_ref[...], mod_ref[0:1, :], mod_ref[1:2, :])
    kv = _dot(h.astype(BF16), w_ref[...])
    k = kv[:, :LANES]
    ss = _dot((k * k).astype(BF16), seg_ref[...])
    k = k * lax.rsqrt(ss * (1.0 / HEAD_DIM) + NORM_EPS) * knw_ref[...]
    k0, k1 = _dup_heads(k)
    v0, v1 = _dup_heads(kv[:, LANES:])
    k_ref[:, :LANES] = k0.astype(BF16)
    k_ref[:, LANES:] = k1.astype(BF16)
    v_ref[:, :LANES] = v0.astype(BF16)
    v_ref[:, LANES:] = v1.astype(BF16)


def _ctx_call(ctx, mod3, norm_w, w_kv, knw2, seg2):
    b, n_ctx, d = ctx.shape
    out = jax.ShapeDtypeStruct((b, n_ctx, 2 * LANES), BF16)
    return pl.pallas_call(
        _ctx_kernel,
        grid=(b,),
        in_specs=[
            pl.BlockSpec((None, n_ctx, d), lambda i: (i, 0, 0)),
            pl.BlockSpec((None, 6, d), lambda i: (b, 0, 0)),
            pl.BlockSpec((1, d), lambda i: (0, 0)),
            pl.BlockSpec((d, 2 * LANES), lambda i: (0, 0)),
            pl.BlockSpec((1, LANES), lambda i: (0, 0)),
            pl.BlockSpec((LANES, LANES), lambda i: (0, 0)),
        ],
        out_specs=[pl.BlockSpec((None, n_ctx, 2 * LANES), lambda i: (i, 0, 0))] * 2,
        out_shape=[out, out],
        name="ctx",
    )(ctx, mod3, norm_w, w_kv, knw2, seg2)


def _rope(x, cos, sin_signed, first_half):
    rot = jnp.where(first_half,
                    pltpu.roll(x, LANES - ROPE_AXIS_DIM // 2, axis=1),
                    pltpu.roll(x, ROPE_AXIS_DIM // 2, axis=1))
    return x * cos + rot * sin_signed


def _proj_kernel(x_ref, mod_ref, nw_ref, w_ref, qknw_ref, cos_ref, sin_ref, seg_ref,
                 q_ref, k_ref, v_ref, u_ref):
    qk_cols = (N_Q_HEADS + N_KV_HEADS) * HEAD_DIM
    kv_cols = N_KV_HEADS * HEAD_DIM
    h = _rms_modulate(x_ref[...], nw_ref[...], mod_ref[0:1, :], mod_ref[1:2, :]).astype(BF16)

    qk = _dot(h, w_ref[:, :qk_cols])
    ss = _dot((qk * qk).astype(BF16), seg_ref[...])
    qk = qk * lax.rsqrt(ss * (1.0 / HEAD_DIM) + NORM_EPS) * qknw_ref[...]
    cos = cos_ref[...]
    sin = sin_ref[...]
    lane = lax.broadcasted_iota(jnp.int32, (1, LANES), 1)
    first_half = (lane % ROPE_AXIS_DIM) < (ROPE_AXIS_DIM // 2)
    n_q_chunks = N_Q_HEADS * HEAD_DIM // LANES
    for c in range(n_q_chunks):
        q_ref[:, c * LANES:(c + 1) * LANES] = _rope(
            qk[:, c * LANES:(c + 1) * LANES], cos, sin, first_half).astype(BF16)
    k0, k1 = _dup_heads(_rope(qk[:, n_q_chunks * LANES:], cos, sin, first_half))
    k_ref[:, :LANES] = k0.astype(BF16)
    k_ref[:, LANES:] = k1.astype(BF16)

    v0, v1 = _dup_heads(_dot(h, w_ref[:, qk_cols:qk_cols + kv_cols]))
    v_ref[:, :LANES] = v0.astype(BF16)
    v_ref[:, LANES:] = v1.astype(BF16)

    glu = _dot(h, w_ref[:, qk_cols + kv_cols:])
    c_half = glu.shape[1] // 2
    u_ref[...] = (glu[:, :c_half] * _sigmoid(glu[:, c_half:])).astype(BF16)


def _proj_call(x, mod3, norm_w, w_in, qknw, cos, sin, seg):
    b, s, d = x.shape
    n_cols = w_in.shape[1]
    qk_cols = (N_Q_HEADS + N_KV_HEADS) * HEAD_DIM
    c_half = (n_cols - qk_cols - N_KV_HEADS * HEAD_DIM) // 2
    tiles = s // PROJ_ROWS
    const = lambda i, j: (0, 0)
    tok = lambda i, j: (i, j, 0)
    return pl.pallas_call(
        _proj_kernel,
        grid=(b, tiles),
        in_specs=[
            pl.BlockSpec((None, PROJ_ROWS, d), tok),
            pl.BlockSpec((None, 6, d), lambda i, j: (i, 0, 0)),
            pl.BlockSpec((1, d), const),
            pl.BlockSpec((d, n_cols), const),
            pl.BlockSpec((1, qk_cols), const),
            pl.BlockSpec((PROJ_ROWS, LANES), lambda i, j: (j, 0)),
            pl.BlockSpec((PROJ_ROWS, LANES), lambda i, j: (j, 0)),
            pl.BlockSpec((qk_cols, qk_cols), const),
        ],
        out_specs=[
            pl.BlockSpec((None, PROJ_ROWS, N_Q_HEADS * HEAD_DIM), tok),
            pl.BlockSpec((None, PROJ_ROWS, 2 * LANES), tok),
            pl.BlockSpec((None, PROJ_ROWS, 2 * LANES), tok),
            pl.BlockSpec((None, PROJ_ROWS, c_half), tok),
        ],
        out_shape=[
            jax.ShapeDtypeStruct((b, s, N_Q_HEADS * HEAD_DIM), BF16),
            jax.ShapeDtypeStruct((b, s, 2 * LANES), BF16),
            jax.ShapeDtypeStruct((b, s, 2 * LANES), BF16),
            jax.ShapeDtypeStruct((b, s, c_half), BF16),
        ],
        compiler_params=pltpu.CompilerParams(dimension_semantics=("arbitrary", "arbitrary")),
        name="proj",
    )(x, mod3, norm_w, w_in, qknw, cos, sin, seg)


def _attend(q_half, k_loc, k_ctx, v_loc, v_ctx, valid, sink):
    s_loc = jnp.where(valid, _dot_t(q_half, k_loc), NEG_INF)
    s_ctx = _dot_t(q_half, k_ctx)
    m = jnp.maximum(jnp.maximum(jnp.max(s_loc, axis=-1, keepdims=True),
                                jnp.max(s_ctx, axis=-1, keepdims=True)), sink)
    p_loc = jnp.exp(s_loc - m)
    p_ctx = jnp.exp(s_ctx - m)
    den = (jnp.sum(p_loc, axis=-1, keepdims=True) + jnp.sum(p_ctx, axis=-1, keepdims=True)
           + jnp.exp(sink - m))
    o = _dot(p_loc.astype(BF16), v_loc) + _dot(p_ctx.astype(BF16), v_ctx)
    return o * (1.0 / den)


def _mix_kernel(sink_ref, x_ref, mod_ref, q_ref, k_ref, v_ref, kc_ref, vc_ref, u_ref,
                cw_ref, cb_ref, lnw_ref, lnb_ref, wout_ref, o_ref, ext_ref, cat_ref):
    s = k_ref.shape[0]
    n_keys = 3 * BLOCK
    attn_cols = N_Q_HEADS * HEAD_DIM
    j = pl.program_id(1)
    t0 = j * MIX_ROWS
    lt64 = _lane_lt64()
    zero = jnp.zeros((), BF16)

    for sb in range(MIX_ROWS // BLOCK):
        q0 = t0 + sb * BLOCK
        w0 = pl.multiple_of(jnp.clip(q0 - BLOCK, 0, s - n_keys), BLOCK)
        rel = (lax.broadcasted_iota(jnp.int32, (BLOCK, n_keys), 0)
               - lax.broadcasted_iota(jnp.int32, (BLOCK, n_keys), 1) + (q0 - w0))
        valid = jnp.abs(rel) <= WINDOW
        for hk in range(N_KV_HEADS):
            cols = slice(hk * LANES, (hk + 1) * LANES)
            k_loc = k_ref[pl.ds(w0, n_keys), cols]
            v_loc = v_ref[pl.ds(w0, n_keys), cols]
            k_ctx = kc_ref[:, cols]
            v_ctx = vc_ref[:, cols]
            v_loc_a, v_loc_b = jnp.where(lt64, v_loc, zero), jnp.where(lt64, zero, v_loc)
            v_ctx_a, v_ctx_b = jnp.where(lt64, v_ctx, zero), jnp.where(lt64, zero, v_ctx)
            chunks_per_kv = attn_cols // LANES // N_KV_HEADS
            for c in range(hk * chunks_per_kv, (hk + 1) * chunks_per_kv):
                q = q_ref[sb * BLOCK:(sb + 1) * BLOCK, c * LANES:(c + 1) * LANES]
                o_a = _attend(jnp.where(lt64, q, zero), k_loc, k_ctx, v_loc_a, v_ctx_a,
                              valid, sink_ref[2 * c])
                o_b = _attend(jnp.where(lt64, zero, q), k_loc, k_ctx, v_loc_b, v_ctx_b,
                              valid, sink_ref[2 * c + 1])
                cat_ref[sb * BLOCK:(sb + 1) * BLOCK, c * LANES:(c + 1) * LANES] = (
                    (o_a + o_b).astype(BF16))

    prev = u_ref[pl.ds(pl.multiple_of(jnp.maximum(t0 - CONV_HALO, 0), CONV_HALO), CONV_HALO), :]
    nxt = u_ref[pl.ds(pl.multiple_of(jnp.minimum(t0 + MIX_ROWS, s - CONV_HALO), CONV_HALO),
                      CONV_HALO), :]
    ext_ref[0:CONV_HALO, :] = jnp.where(j > 0, prev.astype(F32), 0.0)
    ext_ref[CONV_HALO:CONV_HALO + MIX_ROWS, :] = (
        u_ref[pl.ds(pl.multiple_of(t0, MIX_ROWS), MIX_ROWS), :].astype(F32))
    ext_ref[CONV_HALO + MIX_ROWS:, :] = jnp.where(j < pl.num_programs(1) - 1, nxt.astype(F32), 0.0)
    pad = (CONV_KERNEL - 1) // 2
    for rc in range(MIX_ROWS // CONV_ROWS):
        base = rc * CONV_ROWS + CONV_HALO - pad
        acc = ext_ref[base:base + CONV_ROWS, :] * cw_ref[0:1, :] + cb_ref[...]
        for tap in range(1, CONV_KERNEL):
            acc = acc + ext_ref[base + tap:base + tap + CONV_ROWS, :] * cw_ref[tap:tap + 1, :]
        mu = jnp.mean(acc, axis=-1, keepdims=True)
        cen = acc - mu
        var = jnp.mean(cen * cen, axis=-1, keepdims=True)
        y = cen * lax.rsqrt(var + NORM_EPS) * lnw_ref[...] + lnb_ref[...]
        cat_ref[rc * CONV_ROWS:(rc + 1) * CONV_ROWS, attn_cols:] = (y * _sigmoid(y)).astype(BF16)

    mix = _dot(cat_ref[...], wout_ref[...])
    o_ref[...] = x_ref[...] + mod_ref[2:3, :] * mix


def _mix_call(sink, x, mod3, q, k, v, kc, vc, u, conv_w, conv_b, ln_w, ln_b, w_out):
    b, s, d = x.shape
    n_ctx = kc.shape[1]
    cw = u.shape[2]
    tiles = s // MIX_ROWS
    const = lambda i, j: (0, 0)
    tok = lambda i, j: (i, j, 0)
    per_batch = lambda i, j: (i, 0, 0)
    return pl.pallas_call(
        _mix_kernel,
        grid=(b, tiles),
        in_specs=[
            pl.BlockSpec(memory_space=pltpu.SMEM),
            pl.BlockSpec((None, MIX_ROWS, d), tok),
            pl.BlockSpec((None, 6, d), per_batch),
            pl.BlockSpec((None, MIX_ROWS, q.shape[2]), tok),
            pl.BlockSpec((None, s, 2 * LANES), per_batch),
            pl.BlockSpec((None, s, 2 * LANES), per_batch),
            pl.BlockSpec((None, n_ctx, 2 * LANES), per_batch),
            pl.BlockSpec((None, n_ctx, 2 * LANES), per_batch),
            pl.BlockSpec((None, s, cw), per_batch),
            pl.BlockSpec((CONV_KERNEL, cw), const),
            pl.BlockSpec((1, cw), const),
            pl.BlockSpec((1, cw), const),
            pl.BlockSpec((1, cw), const),
            pl.BlockSpec((q.shape[2] + cw, d), const),
        ],
        out_specs=pl.BlockSpec((None, MIX_ROWS, d), tok),
        out_shape=jax.ShapeDtypeStruct((b, s, d), F32),
        scratch_shapes=[
            pltpu.VMEM((MIX_ROWS + 2 * CONV_HALO, cw), F32),
            pltpu.VMEM((MIX_ROWS, q.shape[2] + cw), BF16),
        ],
        compiler_params=pltpu.CompilerParams(dimension_semantics=("arbitrary", "arbitrary")),
        name="mix",
    )(sink, x, mod3, q, k, v, kc, vc, u, conv_w, conv_b, ln_w, ln_b, w_out)


def _ffn_kernel(x_ref, xp_ref, xn_ref, mod_ref, nw_ref, wup_ref, cw_ref, cb_ref, wdn_ref, o_ref):
    j = pl.program_id(1)
    nw = nw_ref[...]
    shift = mod_ref[3:4, :]
    scale = mod_ref[4:5, :]
    x = x_ref[...]
    hp = jnp.where(j > 0, _rms_modulate(xp_ref[...], nw, shift, scale), 0.0)
    hn = jnp.where(j < pl.num_programs(1) - 1, _rms_modulate(xn_ref[...], nw, shift, scale), 0.0)
    h = jnp.concatenate([hp, _rms_modulate(x, nw, shift, scale), hn], axis=0).astype(BF16)
    up = _dot(h, wup_ref[...])
    t = x.shape[0]
    y = (up[FFN_HALO - 1:FFN_HALO - 1 + t] * cw_ref[0:1, :]
         + up[FFN_HALO:FFN_HALO + t] * cw_ref[1:2, :]
         + up[FFN_HALO + 1:FFN_HALO + 1 + t] * cw_ref[2:3, :]
         + cb_ref[...])
    f = y.shape[1] // 2
    gate = y[:, :f]
    act = (gate * _sigmoid(gate) * y[:, f:]).astype(BF16)
    o_ref[...] = x + mod_ref[5:6, :] * _dot(act, wdn_ref[...])


def _ffn_call(x, mod3, norm_w, w_up, conv_w, conv_b, w_down):
    b, s, d = x.shape
    f2 = w_up.shape[1]
    tiles = s // FFN_ROWS
    halo_per_tile = FFN_ROWS // FFN_HALO
    n_halo = s // FFN_HALO
    const = lambda i, j: (0, 0)
    tok = lambda i, j: (i, j, 0)
    return pl.pallas_call(
        _ffn_kernel,
        grid=(b, tiles),
        in_specs=[
            pl.BlockSpec((None, FFN_ROWS, d), tok),
            pl.BlockSpec((None, FFN_HALO, d),
                         lambda i, j: (i, jnp.maximum(j * halo_per_tile - 1, 0), 0)),
            pl.BlockSpec((None, FFN_HALO, d),
                         lambda i, j: (i, jnp.minimum((j + 1) * halo_per_tile, n_halo - 1), 0)),
            pl.BlockSpec((None, 6, d), lambda i, j: (i, 0, 0)),
            pl.BlockSpec((1, d), const),
            pl.BlockSpec((d, f2), const),
            pl.BlockSpec((3, f2), const),
            pl.BlockSpec((1, f2), const),
            pl.BlockSpec((f2 // 2, d), const),
        ],
        out_specs=pl.BlockSpec((None, FFN_ROWS, d), tok),
        out_shape=jax.ShapeDtypeStruct((b, s, d), F32),
        compiler_params=pltpu.CompilerParams(dimension_semantics=("arbitrary", "arbitrary")),
        name="ffn",
    )(x, x, x, mod3, norm_w, w_up, conv_w, conv_b, w_down)


def _rope_tables(s):
    t = jnp.arange(s)
    inv = ROPE_BASE ** (-jnp.arange(0, ROPE_AXIS_DIM, 2, dtype=F32) / ROPE_AXIS_DIM)
    ang_r = (t // GRID_W).astype(F32)[:, None] * inv
    ang_c = (t % GRID_W).astype(F32)[:, None] * inv
    ang = jnp.concatenate([ang_r, ang_r, ang_c, ang_c], axis=-1)
    half = ROPE_AXIS_DIM // 2
    sign = jnp.where((jnp.arange(HEAD_DIM) % ROPE_AXIS_DIM) < half, -1.0, 1.0).astype(F32)
    reps = LANES // HEAD_DIM
    return jnp.tile(jnp.cos(ang), (1, reps)), jnp.tile(jnp.sin(ang) * sign, (1, reps))


def _segment_ones(n):
    g = jnp.arange(n) // HEAD_DIM
    return (g[:, None] == g[None, :]).astype(BF16)


def kernel(x, c, ctx, c_ctx, w_mod, b_mod, norm_mix_w, w_in, q_norm_w, k_norm_w, sink_logit,
           conv_w, conv_b, conv_norm_w, conv_norm_b, w_out, norm_ffn_w, w_up, ffn_conv_w,
           ffn_conv_b, w_down):
    b, s, d = x.shape
    assert w_mod.shape[0] == 1, "single layer"
    q_cols = N_Q_HEADS * HEAD_DIM
    kv_cols = N_KV_HEADS * HEAD_DIM

    cc = jnp.zeros((MOD_ROWS, d), F32).at[:b].set(c).at[b].set(c_ctx)
    mod3 = _mod_call(cc, w_mod[0], b_mod).reshape(MOD_ROWS, 6, d)

    w_in_b = w_in[0].astype(BF16)
    cos, sin = _rope_tables(s)
    scale = 1.0 / math.sqrt(HEAD_DIM)
    qknw = jnp.concatenate([jnp.tile(q_norm_w[0] * scale, N_Q_HEADS),
                            jnp.tile(k_norm_w[0], N_KV_HEADS)])[None, :]
    norm_mix = norm_mix_w[0][None, :]

    kc, vc = _ctx_call(ctx, mod3, norm_mix, w_in_b[:, q_cols:q_cols + 2 * kv_cols],
                       jnp.tile(k_norm_w[0], N_KV_HEADS)[None, :], _segment_ones(kv_cols))
    q, k, v, u = _proj_call(x, mod3, norm_mix, w_in_b, qknw, cos, sin,
                            _segment_ones(q_cols + kv_cols))
    x1 = _mix_call(sink_logit[0], x, mod3, q, k, v, kc, vc, u, conv_w[0], conv_b,
                   conv_norm_w, conv_norm_b, w_out[0].astype(BF16))
    return _ffn_call(x1, mod3, norm_ffn_w[0][None, :], w_up[0].astype(BF16), ffn_conv_w[0],
                     ffn_conv_b, w_down[0].astype(BF16))
```

```python
import math

import jax
import jax.numpy as jnp
from jax import lax
from jax.experimental import pallas as pl
from jax.experimental.pallas import tpu as pltpu

F32 = jnp.float32
BF16 = jnp.bfloat16

LANES = 128
SUBLANES = 8
MXU_DIM = 256
ROW_STRIDE = 4
ROW_GROUP = SUBLANES * ROW_STRIDE

GRID_W = 64
HEAD_DIM = 64
N_Q_HEADS = 8
N_KV_HEADS = 2
GQA_GROUP = N_Q_HEADS // N_KV_HEADS
WINDOW = 128
BLOCK = 128
CONV_KERNEL = 31
CONV_HALO = 16
FFN_CONV = 3
FFN_HALO = 8
ROPE_BASE = 10000.0
ROPE_AXIS_DIM = HEAD_DIM // 2
NORM_EPS = 1e-6
NEG_INF = -1e30

MOD_ROWS = 16
MOD_COLS = 1536
PROJ_ROWS = 512
MIX_ROWS = 256
FFN_ROWS = 512
FFN_CHUNK = MXU_DIM


def _sigmoid(x):
    return 1.0 / (1.0 + jnp.exp(-x))


def _dot(a, b):
    return jnp.dot(a, b, preferred_element_type=F32)


def _dot_t(a, b):
    return lax.dot_general(a, b, (((1,), (1,)), ((), ())), preferred_element_type=F32)


def _rms_modulate(x, gain, shift):
    r = lax.rsqrt(jnp.mean(x * x, axis=-1, keepdims=True) + NORM_EPS)
    return (x * r) * gain + shift


def _lane_lt64():
    return lax.broadcasted_iota(jnp.int32, (1, LANES), 1) < HEAD_DIM


def _dup_heads(x):
    lt = _lane_lt64()
    sw = pltpu.roll(x, HEAD_DIM, axis=1)
    return jnp.where(lt, x, sw), jnp.where(lt, sw, x)


def _rows(v):
    return jnp.broadcast_to(v, (SUBLANES, LANES))


def _mod_kernel(c_ref, w_ref, b_ref, o_ref):
    cc = c_ref[...]
    a = (cc * _sigmoid(cc)).astype(BF16)
    o_ref[...] = _dot(a, w_ref[...].astype(BF16)) + b_ref[...]


def _mod_call(cc, w_mod, b_mod):
    d, n = w_mod.shape
    return pl.pallas_call(
        _mod_kernel,
        grid=(n // MOD_COLS,),
        in_specs=[
            pl.BlockSpec((MOD_ROWS, d), lambda j: (0, 0)),
            pl.BlockSpec((d, MOD_COLS), lambda j: (0, j)),
            pl.BlockSpec((1, MOD_COLS), lambda j: (0, j)),
        ],
        out_specs=pl.BlockSpec((MOD_ROWS, MOD_COLS), lambda j: (0, j)),
        out_shape=jax.ShapeDtypeStruct((MOD_ROWS, n), F32),
        name="mod",
    )(cc, w_mod, b_mod)


def _ctx_kernel(x_ref, mod_ref, nw_ref, w_ref, knw_ref, seg_ref, k_ref, v_ref):
    gain = nw_ref[...] * (1.0 + mod_ref[1:2, :])
    h = _rms_modulate(x_ref[...], gain, mod_ref[0:1, :])
    kv = _dot(h.astype(BF16), w_ref[...])
    k = kv[:, :LANES]
    ss = _dot((k * k).astype(BF16), seg_ref[:LANES, :LANES])
    k = k * lax.rsqrt(ss * (1.0 / HEAD_DIM) + NORM_EPS) * knw_ref[...]
    k0, k1 = _dup_heads(k)
    v0, v1 = _dup_heads(kv[:, LANES:])
    k_ref[:, :LANES] = k0.astype(BF16)
    k_ref[:, LANES:] = k1.astype(BF16)
    v_ref[:, :LANES] = v0.astype(BF16)
    v_ref[:, LANES:] = v1.astype(BF16)


def _ctx_call(ctx, mod3, norm_w, w_kv, knw2, seg):
    b, n_ctx, d = ctx.shape
    out = jax.ShapeDtypeStruct((b, n_ctx, 2 * LANES), BF16)
    return pl.pallas_call(
        _ctx_kernel,
        grid=(b,),
        in_specs=[
            pl.BlockSpec((None, n_ctx, d), lambda i: (i, 0, 0)),
            pl.BlockSpec((None, 6, d), lambda i: (b, 0, 0)),
            pl.BlockSpec((1, d), lambda i: (0, 0)),
            pl.BlockSpec((d, 2 * LANES), lambda i: (0, 0)),
            pl.BlockSpec((1, LANES), lambda i: (0, 0)),
            pl.BlockSpec((MXU_DIM, MXU_DIM), lambda i: (0, 0)),
        ],
        out_specs=[pl.BlockSpec((None, n_ctx, 2 * LANES), lambda i: (i, 0, 0))] * 2,
        out_shape=[out, out],
        name="ctx",
    )(ctx, mod3, norm_w, w_kv, knw2, seg)


def _rope(x, cos, sin_signed, first_half):
    rot = jnp.where(first_half,
                    pltpu.roll(x, LANES - ROPE_AXIS_DIM // 2, axis=1),
                    pltpu.roll(x, ROPE_AXIS_DIM // 2, axis=1))
    return x * cos + rot * sin_signed


def _proj_kernel(x_ref, mod_ref, nw_ref, w_ref, qknw_ref, cos_ref, sin_ref, seg_ref,
                 q_ref, k_ref, v_ref, u_ref):
    q_cols = N_Q_HEADS * HEAD_DIM
    kv_cols = N_KV_HEADS * HEAD_DIM
    qk_cols = q_cols + kv_cols
    gain = nw_ref[...] * (1.0 + mod_ref[1:2, :])
    h = _rms_modulate(x_ref[...], gain, mod_ref[0:1, :]).astype(BF16)

    qk = _dot(h, w_ref[:, :qk_cols])
    sq = (qk * qk).astype(BF16)
    ss = jnp.concatenate(
        [_dot(sq[:, c:c + MXU_DIM], seg_ref[...]) for c in range(0, q_cols, MXU_DIM)]
        + [_dot(sq[:, q_cols:], seg_ref[:kv_cols, :kv_cols])], axis=1)
    qk = qk * lax.rsqrt(ss * (1.0 / HEAD_DIM) + NORM_EPS) * qknw_ref[...]
    cos = cos_ref[...]
    sin = sin_ref[...]
    lane = lax.broadcasted_iota(jnp.int32, (1, LANES), 1)
    first_half = (lane % ROPE_AXIS_DIM) < (ROPE_AXIS_DIM // 2)
    for c in range(q_cols // LANES):
        q_ref[:, c * LANES:(c + 1) * LANES] = _rope(
            qk[:, c * LANES:(c + 1) * LANES], cos, sin, first_half).astype(BF16)
    k0, k1 = _dup_heads(_rope(qk[:, q_cols:], cos, sin, first_half))
    k_ref[:, :LANES] = k0.astype(BF16)
    k_ref[:, LANES:] = k1.astype(BF16)

    v0, v1 = _dup_heads(_dot(h, w_ref[:, qk_cols:qk_cols + kv_cols]))
    v_ref[:, :LANES] = v0.astype(BF16)
    v_ref[:, LANES:] = v1.astype(BF16)

    glu = _dot(h, w_ref[:, qk_cols + kv_cols:])
    c_half = glu.shape[1] // 2
    u_ref[...] = (glu[:, :c_half] * _sigmoid(glu[:, c_half:])).astype(BF16)


def _proj_call(x, mod3, norm_w, w_in, qknw, cos, sin, seg):
    b, s, d = x.shape
    n_cols = w_in.shape[1]
    q_cols = N_Q_HEADS * HEAD_DIM
    qk_cols = q_cols + N_KV_HEADS * HEAD_DIM
    c_half = (n_cols - qk_cols - N_KV_HEADS * HEAD_DIM) // 2
    tiles = s // PROJ_ROWS
    const = lambda i, j: (0, 0)
    tok = lambda i, j: (i, j, 0)
    return pl.pallas_call(
        _proj_kernel,
        grid=(b, tiles),
        in_specs=[
            pl.BlockSpec((None, PROJ_ROWS, d), tok),
            pl.BlockSpec((None, 6, d), lambda i, j: (i, 0, 0)),
            pl.BlockSpec((1, d), const),
            pl.BlockSpec((d, n_cols), const),
            pl.BlockSpec((1, qk_cols), const),
            pl.BlockSpec((PROJ_ROWS, LANES), lambda i, j: (j, 0)),
            pl.BlockSpec((PROJ_ROWS, LANES), lambda i, j: (j, 0)),
            pl.BlockSpec((MXU_DIM, MXU_DIM), const),
        ],
        out_specs=[
            pl.BlockSpec((None, PROJ_ROWS, q_cols), tok),
            pl.BlockSpec((None, PROJ_ROWS, 2 * LANES), tok),
            pl.BlockSpec((None, PROJ_ROWS, 2 * LANES), tok),
            pl.BlockSpec((None, PROJ_ROWS, c_half), tok),
        ],
        out_shape=[
            jax.ShapeDtypeStruct((b, s, q_cols), BF16),
            jax.ShapeDtypeStruct((b, s, 2 * LANES), BF16),
            jax.ShapeDtypeStruct((b, s, 2 * LANES), BF16),
            jax.ShapeDtypeStruct((b, s, c_half), BF16),
        ],
        compiler_params=pltpu.CompilerParams(dimension_semantics=("arbitrary", "arbitrary")),
        name="proj",
    )(x, mod3, norm_w, w_in, qknw, cos, sin, seg)


def _attend_group(q4, k_loc, k_ctx, v_loc, v_ctx, bias4, sink4):
    s_loc = _dot_t(q4, k_loc) + bias4
    s_ctx = _dot_t(q4, k_ctx)
    m = jnp.maximum(jnp.maximum(jnp.max(s_loc, axis=-1, keepdims=True),
                                jnp.max(s_ctx, axis=-1, keepdims=True)), sink4)
    p_loc = jnp.exp(s_loc - m)
    p_ctx = jnp.exp(s_ctx - m)
    den = (jnp.sum(p_loc, axis=-1, keepdims=True) + jnp.sum(p_ctx, axis=-1, keepdims=True)
           + jnp.exp(sink4 - m))
    o = _dot(p_loc.astype(BF16), v_loc) + _dot(p_ctx.astype(BF16), v_ctx)
    return o * (1.0 / den)


def _mix_kernel(sink_ref, x_ref, mod_ref, q_ref, k_ref, v_ref, kc_ref, vc_ref, u_ref,
                cw_ref, cb_ref, lnw_ref, lnb_ref, wout_ref, o_ref, ext_ref, y_ref, cat_ref):
    s = k_ref.shape[0]
    n_keys = 3 * BLOCK
    attn_cols = N_Q_HEADS * HEAD_DIM
    conv_slabs = ext_ref.shape[0]
    j = pl.program_id(1)
    t0 = j * MIX_ROWS
    lt64 = _lane_lt64()
    zero = jnp.zeros((), BF16)

    chunks_per_kv = attn_cols // LANES // N_KV_HEADS
    for sb in range(MIX_ROWS // BLOCK):
        q0 = t0 + sb * BLOCK
        w0 = pl.multiple_of(jnp.clip(q0 - BLOCK, 0, s - n_keys), BLOCK)
        rel = (lax.broadcasted_iota(jnp.int32, (BLOCK, n_keys), 0)
               - lax.broadcasted_iota(jnp.int32, (BLOCK, n_keys), 1) + (q0 - w0))
        bias = jnp.where(jnp.abs(rel) <= WINDOW, 0.0, NEG_INF)
        bias4 = jnp.concatenate([bias] * GQA_GROUP, axis=0)
        for hk in range(N_KV_HEADS):
            cols = slice(hk * LANES, (hk + 1) * LANES)
            q_parts = []
            sink_parts = []
            for c in range(hk * chunks_per_kv, (hk + 1) * chunks_per_kv):
                q = q_ref[sb * BLOCK:(sb + 1) * BLOCK, c * LANES:(c + 1) * LANES]
                q_parts += [jnp.where(lt64, q, zero), jnp.where(lt64, zero, q)]
                sink_parts += [jnp.full((BLOCK, 1), sink_ref[2 * c], F32),
                               jnp.full((BLOCK, 1), sink_ref[2 * c + 1], F32)]
            o4 = _attend_group(jnp.concatenate(q_parts, axis=0),
                               k_ref[pl.ds(w0, n_keys), cols], kc_ref[:, cols],
                               v_ref[pl.ds(w0, n_keys), cols], vc_ref[:, cols],
                               bias4, jnp.concatenate(sink_parts, axis=0))
            for ci in range(chunks_per_kv):
                c = hk * chunks_per_kv + ci
                o_a = o4[(2 * ci) * BLOCK:(2 * ci + 1) * BLOCK]
                o_b = o4[(2 * ci + 1) * BLOCK:(2 * ci + 2) * BLOCK]
                cat_ref[sb * BLOCK:(sb + 1) * BLOCK, c * LANES:(c + 1) * LANES] = (
                    jnp.where(lt64, o_a, o_b).astype(BF16))

    prev = u_ref[pl.ds(pl.multiple_of(jnp.maximum(t0 - CONV_HALO, 0), CONV_HALO), CONV_HALO), :]
    nxt = u_ref[pl.ds(pl.multiple_of(jnp.minimum(t0 + MIX_ROWS, s - CONV_HALO), CONV_HALO),
                      CONV_HALO), :]
    prev = jnp.where(j > 0, prev.astype(F32), 0.0)
    nxt = jnp.where(j < pl.num_programs(1) - 1, nxt.astype(F32), 0.0)
    main = u_ref[pl.ds(pl.multiple_of(t0, MIX_ROWS), MIX_ROWS), :].astype(F32)
    for l in range(conv_slabs):
        ls = slice(l * LANES, (l + 1) * LANES)
        ext_ref[l, 0:CONV_HALO, :] = prev[:, ls]
        ext_ref[l, CONV_HALO:CONV_HALO + MIX_ROWS, :] = main[:, ls]
        ext_ref[l, CONV_HALO + MIX_ROWS:, :] = nxt[:, ls]
    pad = (CONV_KERNEL - 1) // 2
    for l in range(conv_slabs):
        ls = slice(l * LANES, (l + 1) * LANES)
        taps = [_rows(cw_ref[t:t + 1, ls]) for t in range(CONV_KERNEL)]
        bias_rows = _rows(cb_ref[:, ls])
        for g in range(MIX_ROWS // ROW_GROUP):
            base = g * ROW_GROUP + CONV_HALO - pad
            ld = [ext_ref[l, pl.ds(base + r, SUBLANES, stride=ROW_STRIDE), :]
                  for r in range(CONV_KERNEL + ROW_STRIDE - 1)]
            for i in range(ROW_STRIDE):
                acc = bias_rows
                for t in range(CONV_KERNEL):
                    acc = acc + ld[i + t] * taps[t]
                y_ref[l, pl.ds(g * ROW_GROUP + i, SUBLANES, stride=ROW_STRIDE), :] = acc
    yv = jnp.concatenate([y_ref[l] for l in range(conv_slabs)], axis=1)
    mu = jnp.mean(yv, axis=-1, keepdims=True)
    cen = yv - mu
    var = jnp.mean(cen * cen, axis=-1, keepdims=True)
    yn = cen * lax.rsqrt(var + NORM_EPS) * lnw_ref[...] + lnb_ref[...]
    cat_ref[:, attn_cols:] = (yn * _sigmoid(yn)).astype(BF16)

    mix = _dot(cat_ref[...], wout_ref[...])
    o_ref[...] = x_ref[...] + mod_ref[2:3, :] * mix


def _mix_call(sink, x, mod3, q, k, v, kc, vc, u, conv_w, conv_b, ln_w, ln_b, w_out):
    b, s, d = x.shape
    n_ctx = kc.shape[1]
    cw = u.shape[2]
    tiles = s // MIX_ROWS
    const = lambda i, j: (0, 0)
    tok = lambda i, j: (i, j, 0)
    per_batch = lambda i, j: (i, 0, 0)
    return pl.pallas_call(
        _mix_kernel,
        grid=(b, tiles),
        in_specs=[
            pl.BlockSpec(memory_space=pltpu.SMEM),
            pl.BlockSpec((None, MIX_ROWS, d), tok),
            pl.BlockSpec((None, 6, d), per_batch),
            pl.BlockSpec((None, MIX_ROWS, q.shape[2]), tok),
            pl.BlockSpec((None, s, 2 * LANES), per_batch),
            pl.BlockSpec((None, s, 2 * LANES), per_batch),
            pl.BlockSpec((None, n_ctx, 2 * LANES), per_batch),
            pl.BlockSpec((None, n_ctx, 2 * LANES), per_batch),
            pl.BlockSpec((None, s, cw), per_batch),
            pl.BlockSpec((CONV_KERNEL, cw), const),
            pl.BlockSpec((1, cw), const),
            pl.BlockSpec((1, cw), const),
            pl.BlockSpec((1, cw), const),
            pl.BlockSpec((q.shape[2] + cw, d), const),
        ],
        out_specs=pl.BlockSpec((None, MIX_ROWS, d), tok),
        out_shape=jax.ShapeDtypeStruct((b, s, d), F32),
        scratch_shapes=[
            pltpu.VMEM((cw // LANES, MIX_ROWS + 2 * CONV_HALO, LANES), F32),
            pltpu.VMEM((cw // LANES, MIX_ROWS, LANES), F32),
            pltpu.VMEM((MIX_ROWS, q.shape[2] + cw), BF16),
        ],
        compiler_params=pltpu.CompilerParams(dimension_semantics=("arbitrary", "arbitrary")),
        name="mix",
    )(sink, x, mod3, q, k, v, kc, vc, u, conv_w, conv_b, ln_w, ln_b, w_out)


def _conv3_strided(up_ref, slab, w_rows, b_rows, t):
    pieces = []
    for g in range(t // ROW_GROUP):
        base = FFN_HALO - 1 + g * ROW_GROUP
        ld = [up_ref[slab, pl.ds(base + r, SUBLANES, stride=ROW_STRIDE), :]
              for r in range(FFN_CONV + ROW_STRIDE - 1)]
        for i in range(ROW_STRIDE):
            pieces.append(ld[i] * w_rows[0] + ld[i + 1] * w_rows[1] + ld[i + 2] * w_rows[2] + b_rows)
    return jnp.concatenate(pieces, axis=0)


def _ffn_kernel(x_ref, xp_ref, xn_ref, mod_ref, nw_ref, wup_ref, cw_ref, cb_ref, wdn_ref, o_ref,
                up_ref, res_ref):
    j = pl.program_id(1)
    gain = nw_ref[...] * (1.0 + mod_ref[4:5, :])
    shift = mod_ref[3:4, :]
    x = x_ref[...]
    t = x.shape[0]
    f = wdn_ref.shape[0]
    hp = jnp.where(j > 0, _rms_modulate(xp_ref[...], gain, shift), 0.0)
    hn = jnp.where(j < pl.num_programs(1) - 1, _rms_modulate(xn_ref[...], gain, shift), 0.0)
    h = jnp.concatenate([hp, _rms_modulate(x, gain, shift), hn], axis=0).astype(BF16)

    slabs_per_chunk = FFN_CHUNK // LANES
    act_parts = []
    for ch in range(f // FFN_CHUNK):
        halves = []
        for col0 in (ch * FFN_CHUNK, f + ch * FFN_CHUNK):
            up = _dot(h, wup_ref[:, col0:col0 + FFN_CHUNK])
            outs = []
            for sl in range(slabs_per_chunk):
                slab = col0 // LANES + sl
                cs = slice(slab * LANES, (slab + 1) * LANES)
                up_ref[slab] = up[:, sl * LANES:(sl + 1) * LANES]
                w_rows = [_rows(cw_ref[k:k + 1, cs]) for k in range(FFN_CONV)]
                outs.append(_conv3_strided(up_ref, slab, w_rows, _rows(cb_ref[:, cs]), t))
            halves.append(jnp.concatenate(outs, axis=1))
        gate, val = halves
        act_parts.append((gate * _sigmoid(gate) * val).astype(BF16))
    res = _dot(jnp.concatenate(act_parts, axis=1), wdn_ref[...])
    for l in range(res_ref.shape[0]):
        for g in range(t // ROW_GROUP):
            for i in range(ROW_STRIDE):
                r0 = g * ROW_GROUP + i * SUBLANES
                res_ref[l, pl.ds(g * ROW_GROUP + i, SUBLANES, stride=ROW_STRIDE), :] = (
                    res[r0:r0 + SUBLANES, l * LANES:(l + 1) * LANES])
    res_nat = jnp.concatenate([res_ref[l] for l in range(res_ref.shape[0])], axis=1)
    o_ref[...] = x + mod_ref[5:6, :] * res_nat


def _ffn_call(x, mod3, norm_w, w_up, conv_w, conv_b, w_down):
    b, s, d = x.shape
    f2 = w_up.shape[1]
    tiles = s // FFN_ROWS
    halo_per_tile = FFN_ROWS // FFN_HALO
    n_halo = s // FFN_HALO
    const = lambda i, j: (0, 0)
    tok = lambda i, j: (i, j, 0)
    once = pl.Buffered(1)
    return pl.pallas_call(
        _ffn_kernel,
        grid=(b, tiles),
        in_specs=[
            pl.BlockSpec((None, FFN_ROWS, d), tok),
            pl.BlockSpec((None, FFN_HALO, d),
                         lambda i, j: (i, jnp.maximum(j * halo_per_tile - 1, 0), 0)),
            pl.BlockSpec((None, FFN_HALO, d),
                         lambda i, j: (i, jnp.minimum((j + 1) * halo_per_tile, n_halo - 1), 0)),
            pl.BlockSpec((None, 6, d), lambda i, j: (i, 0, 0)),
            pl.BlockSpec((1, d), const),
            pl.BlockSpec((d, f2), const, pipeline_mode=once),
            pl.BlockSpec((FFN_CONV, f2), const),
            pl.BlockSpec((1, f2), const),
            pl.BlockSpec((f2 // 2, d), const, pipeline_mode=once),
        ],
        out_specs=pl.BlockSpec((None, FFN_ROWS, d), tok),
        out_shape=jax.ShapeDtypeStruct((b, s, d), F32),
        scratch_shapes=[
            pltpu.VMEM((f2 // LANES, FFN_ROWS + 2 * FFN_HALO, LANES), F32),
            pltpu.VMEM((d // LANES, FFN_ROWS, LANES), F32),
        ],
        compiler_params=pltpu.CompilerParams(dimension_semantics=("arbitrary", "arbitrary")),
        name="ffn",
    )(x, x, x, mod3, norm_w, w_up, conv_w, conv_b, w_down)


def _rope_tables(s):
    t = jnp.arange(s)
    inv = ROPE_BASE ** (-jnp.arange(0, ROPE_AXIS_DIM, 2, dtype=F32) / ROPE_AXIS_DIM)
    ang_r = (t // GRID_W).astype(F32)[:, None] * inv
    ang_c = (t % GRID_W).astype(F32)[:, None] * inv
    ang = jnp.concatenate([ang_r, ang_r, ang_c, ang_c], axis=-1)
    half = ROPE_AXIS_DIM // 2
    sign = jnp.where((jnp.arange(HEAD_DIM) % ROPE_AXIS_DIM) < half, -1.0, 1.0).astype(F32)
    reps = LANES // HEAD_DIM
    return jnp.tile(jnp.cos(ang), (1, reps)), jnp.tile(jnp.sin(ang) * sign, (1, reps))


def _segment_ones(n):
    g = jnp.arange(n) // HEAD_DIM
    return (g[:, None] == g[None, :]).astype(BF16)


def kernel(x, c, ctx, c_ctx, w_mod, b_mod, norm_mix_w, w_in, q_norm_w, k_norm_w, sink_logit,
           conv_w, conv_b, conv_norm_w, conv_norm_b, w_out, norm_ffn_w, w_up, ffn_conv_w,
           ffn_conv_b, w_down):
    b, s, d = x.shape
    assert w_mod.shape[0] == 1, "single layer"
    q_cols = N_Q_HEADS * HEAD_DIM
    kv_cols = N_KV_HEADS * HEAD_DIM

    cc = jnp.zeros((MOD_ROWS, d), F32).at[:b].set(c).at[b].set(c_ctx)
    mod3 = _mod_call(cc, w_mod[0], b_mod).reshape(MOD_ROWS, 6, d)

    w_in_b = w_in[0].astype(BF16)
    cos, sin = _rope_tables(s)
    scale = 1.0 / math.sqrt(HEAD_DIM)
    qknw = jnp.concatenate([jnp.tile(q_norm_w[0] * scale, N_Q_HEADS),
                            jnp.tile(k_norm_w[0], N_KV_HEADS)])[None, :]
    norm_mix = norm_mix_w[0][None, :]
    seg = _segment_ones(MXU_DIM)

    kc, vc = _ctx_call(ctx, mod3, norm_mix, w_in_b[:, q_cols:q_cols + 2 * kv_cols],
                       jnp.tile(k_norm_w[0], N_KV_HEADS)[None, :], seg)
    q, k, v, u = _proj_call(x, mod3, norm_mix, w_in_b, qknw, cos, sin, seg)
    x1 = _mix_call(sink_logit[0], x, mod3, q, k, v, kc, vc, u, conv_w[0], conv_b,
                   conv_norm_w, conv_norm_b, w_out[0].astype(BF16))
    return _ffn_call(x1, mod3, norm_ffn_w[0][None, :], w_up[0].astype(BF16), ffn_conv_w[0],
                     ffn_conv_b, w_down[0].astype(BF16))
```

```python
import math

import jax
import jax.numpy as jnp
from jax import lax
from jax.experimental import pallas as pl
from jax.experimental.pallas import tpu as pltpu

F32 = jnp.float32
BF16 = jnp.bfloat16

LANES = 128
SUBLANES = 8
MXU_DIM = 256
ROW_STRIDE = 4
ROW_GROUP = SUBLANES * ROW_STRIDE

GRID_W = 64
HEAD_DIM = 64
N_Q_HEADS = 8
N_KV_HEADS = 2
GQA_GROUP = N_Q_HEADS // N_KV_HEADS
WINDOW = 128
BLOCK = 128
assert WINDOW == BLOCK
CONV_KERNEL = 31
CONV_HALO = 16
FFN_CONV = 3
FFN_HALO = 8
ROPE_BASE = 10000.0
ROPE_AXIS_DIM = HEAD_DIM // 2
NORM_EPS = 1e-6
NEG_INF = -1e30
LOG2E = math.log2(math.e)

MOD_ROWS = 16
MOD_COLS = 1536
PROJ_ROWS = 512
MIX_ROWS = 256
FFN_ROWS = 512
FFN_CHUNK = MXU_DIM


def _sigmoid(x):
    return 1.0 / (1.0 + jnp.exp(-x))


def _dot(a, b):
    return jnp.dot(a, b, preferred_element_type=F32)


def _dot_t(a, b):
    return lax.dot_general(a, b, (((1,), (1,)), ((), ())), preferred_element_type=F32)


def _rms_modulate(x, gain, shift):
    r = lax.rsqrt(jnp.mean(x * x, axis=-1, keepdims=True) + NORM_EPS)
    return (x * r) * gain + shift


def _lane_lt64():
    return lax.broadcasted_iota(jnp.int32, (1, LANES), 1) < HEAD_DIM


def _dup_heads(x):
    lt = _lane_lt64()
    sw = pltpu.roll(x, HEAD_DIM, axis=1)
    return jnp.where(lt, x, sw), jnp.where(lt, sw, x)


def _rows(v):
    return jnp.broadcast_to(v, (SUBLANES, LANES))


def _mod_kernel(c_ref, w_ref, b_ref, o_ref):
    cc = c_ref[...]
    a = (cc * _sigmoid(cc)).astype(BF16)
    o_ref[...] = _dot(a, w_ref[...].astype(BF16)) + b_ref[...]


def _mod_call(cc, w_mod, b_mod):
    d, n = w_mod.shape
    return pl.pallas_call(
        _mod_kernel,
        grid=(n // MOD_COLS,),
        in_specs=[
            pl.BlockSpec((MOD_ROWS, d), lambda j: (0, 0)),
            pl.BlockSpec((d, MOD_COLS), lambda j: (0, j)),
            pl.BlockSpec((1, MOD_COLS), lambda j: (0, j)),
        ],
        out_specs=pl.BlockSpec((MOD_ROWS, MOD_COLS), lambda j: (0, j)),
        out_shape=jax.ShapeDtypeStruct((MOD_ROWS, n), F32),
        name="mod",
    )(cc, w_mod, b_mod)


def _ctx_kernel(x_ref, mod_ref, nw_ref, w_ref, knw_ref, seg_ref, k_ref, vt_ref):
    gain = nw_ref[...] * (1.0 + mod_ref[1:2, :])
    h = _rms_modulate(x_ref[...], gain, mod_ref[0:1, :])
    kv = _dot(h.astype(BF16), w_ref[...])
    k = kv[:, :LANES]
    ss = _dot((k * k).astype(BF16), seg_ref[:LANES, :LANES])
    k = k * lax.rsqrt(ss * (1.0 / HEAD_DIM) + NORM_EPS) * knw_ref[...]
    k0, k1 = _dup_heads(k)
    k_ref[:, :LANES] = k0.astype(BF16)
    k_ref[:, LANES:] = k1.astype(BF16)
    _store_transposed_blocks(vt_ref, kv[:, LANES:])


def _store_transposed_blocks(vt_ref, v):
    for i in range(vt_ref.shape[0]):
        vt_ref[i] = v[i * BLOCK:(i + 1) * BLOCK, :].T.astype(BF16)


def _ctx_call(ctx, mod3, norm_w, w_kv, knw2, seg):
    b, n_ctx, d = ctx.shape
    return pl.pallas_call(
        _ctx_kernel,
        grid=(b,),
        in_specs=[
            pl.BlockSpec((None, n_ctx, d), lambda i: (i, 0, 0)),
            pl.BlockSpec((None, 6, d), lambda i: (b, 0, 0)),
            pl.BlockSpec((1, d), lambda i: (0, 0)),
            pl.BlockSpec((d, 2 * LANES), lambda i: (0, 0)),
            pl.BlockSpec((1, LANES), lambda i: (0, 0)),
            pl.BlockSpec((MXU_DIM, MXU_DIM), lambda i: (0, 0)),
        ],
        out_specs=[pl.BlockSpec((None, n_ctx, 2 * LANES), lambda i: (i, 0, 0)),
                   pl.BlockSpec((None, n_ctx // BLOCK, LANES, BLOCK), lambda i: (i, 0, 0, 0))],
        out_shape=[jax.ShapeDtypeStruct((b, n_ctx, 2 * LANES), BF16),
                   jax.ShapeDtypeStruct((b, n_ctx // BLOCK, LANES, BLOCK), BF16)],
        name="ctx",
    )(ctx, mod3, norm_w, w_kv, knw2, seg)


def _rope(x, cos, sin_signed, first_half):
    rot = jnp.where(first_half,
                    pltpu.roll(x, LANES - ROPE_AXIS_DIM // 2, axis=1),
                    pltpu.roll(x, ROPE_AXIS_DIM // 2, axis=1))
    return x * cos + rot * sin_signed


def _conv_slab(ext_ref, y_ref, cw_ref, cb_ref, l):
    t = y_ref.shape[1]
    pad = (CONV_KERNEL - 1) // 2
    ls = slice(l * LANES, (l + 1) * LANES)
    bias_rows = _rows(cb_ref[:, ls])
    for g in range(t // ROW_GROUP):
        base = g * ROW_GROUP + CONV_HALO - pad
        acc = [bias_rows] * ROW_STRIDE
        for r in range(CONV_KERNEL + ROW_STRIDE - 1):
            v = ext_ref[l, pl.ds(base + r, SUBLANES, stride=ROW_STRIDE), :]
            for i in range(ROW_STRIDE):
                if 0 <= r - i < CONV_KERNEL:
                    acc[i] = acc[i] + v * _rows(cw_ref[r - i:r - i + 1, ls])
        for i in range(ROW_STRIDE):
            y_ref[l, pl.ds(g * ROW_GROUP + i, SUBLANES, stride=ROW_STRIDE), :] = acc[i]


def _layernorm_silu(y_ref, lnw_ref, lnb_ref):
    y = jnp.concatenate([y_ref[l] for l in range(y_ref.shape[0])], axis=1)
    mu = jnp.mean(y, axis=-1, keepdims=True)
    cen = y - mu
    var = jnp.mean(cen * cen, axis=-1, keepdims=True)
    yn = cen * lax.rsqrt(var + NORM_EPS) * lnw_ref[...] + lnb_ref[...]
    return yn * _sigmoid(yn)


def _proj_kernel(x_ref, mod_ref, nw_ref, w_ref, qknw_ref, cos_ref, sin_ref, seg_ref,
                 q_ref, k_ref, vt_ref, u_ref):
    q_cols = N_Q_HEADS * HEAD_DIM
    kv_cols = N_KV_HEADS * HEAD_DIM
    qk_cols = q_cols + kv_cols
    gain = nw_ref[...] * (1.0 + mod_ref[1:2, :])
    h = _rms_modulate(x_ref[...], gain, mod_ref[0:1, :]).astype(BF16)

    qk = _dot(h, w_ref[:, :qk_cols])
    sq = (qk * qk).astype(BF16)
    ss = jnp.concatenate(
        [_dot(sq[:, c:c + MXU_DIM], seg_ref[...]) for c in range(0, q_cols, MXU_DIM)]
        + [_dot(sq[:, q_cols:], seg_ref[:kv_cols, :kv_cols])], axis=1)
    qk = qk * lax.rsqrt(ss * (1.0 / HEAD_DIM) + NORM_EPS) * qknw_ref[...]
    cos = cos_ref[...]
    sin = sin_ref[...]
    lane = lax.broadcasted_iota(jnp.int32, (1, LANES), 1)
    first_half = (lane % ROPE_AXIS_DIM) < (ROPE_AXIS_DIM // 2)
    for c in range(q_cols // LANES):
        q_ref[:, c * LANES:(c + 1) * LANES] = _rope(
            qk[:, c * LANES:(c + 1) * LANES], cos, sin, first_half).astype(BF16)
    k0, k1 = _dup_heads(_rope(qk[:, q_cols:], cos, sin, first_half))
    k_ref[:, :LANES] = k0.astype(BF16)
    k_ref[:, LANES:] = k1.astype(BF16)

    _store_transposed_blocks(vt_ref, _dot(h, w_ref[:, qk_cols:qk_cols + kv_cols]))

    glu = _dot(h, w_ref[:, qk_cols + kv_cols:])
    c_half = glu.shape[1] // 2
    u_ref[...] = (glu[:, :c_half] * _sigmoid(glu[:, c_half:])).astype(BF16)


def _proj_call(x, mod3, norm_w, w_in, qknw, cos, sin, seg):
    b, s, d = x.shape
    n_cols = w_in.shape[1]
    q_cols = N_Q_HEADS * HEAD_DIM
    qk_cols = q_cols + N_KV_HEADS * HEAD_DIM
    c_half = (n_cols - qk_cols - N_KV_HEADS * HEAD_DIM) // 2
    tiles = s // PROJ_ROWS
    const = lambda i, j: (0, 0)
    tok = lambda i, j: (i, j, 0)
    return pl.pallas_call(
        _proj_kernel,
        grid=(b, tiles),
        in_specs=[
            pl.BlockSpec((None, PROJ_ROWS, d), tok),
            pl.BlockSpec((None, 6, d), lambda i, j: (i, 0, 0)),
            pl.BlockSpec((1, d), const),
            pl.BlockSpec((d, n_cols), const),
            pl.BlockSpec((1, qk_cols), const),
            pl.BlockSpec((PROJ_ROWS, LANES), lambda i, j: (j, 0)),
            pl.BlockSpec((PROJ_ROWS, LANES), lambda i, j: (j, 0)),
            pl.BlockSpec((MXU_DIM, MXU_DIM), const),
        ],
        out_specs=[
            pl.BlockSpec((None, PROJ_ROWS, q_cols), tok),
            pl.BlockSpec((None, PROJ_ROWS, 2 * LANES), tok),
            pl.BlockSpec((None, PROJ_ROWS // BLOCK, LANES, BLOCK), lambda i, j: (i, j, 0, 0)),
            pl.BlockSpec((None, PROJ_ROWS, c_half), tok),
        ],
        out_shape=[
            jax.ShapeDtypeStruct((b, s, q_cols), BF16),
            jax.ShapeDtypeStruct((b, s, 2 * LANES), BF16),
            jax.ShapeDtypeStruct((b, s // BLOCK, LANES, BLOCK), BF16),
            jax.ShapeDtypeStruct((b, s, c_half), BF16),
        ],
        compiler_params=pltpu.CompilerParams(dimension_semantics=("arbitrary", "arbitrary")),
        name="proj",
    )(x, mod3, norm_w, w_in, qknw, cos, sin, seg)


def _mix_kernel(sink_ref, x_ref, mod_ref, q_ref, k_ref, vt_ref, kc_ref, vct_ref, u_ref,
                cw_ref, cb_ref, lnw_ref, lnb_ref, wout_ref, o_ref, s_ref, ext_ref, y_ref, attn_ref):
    s = u_ref.shape[0]
    n_blocks = k_ref.shape[0]
    n_ctx_blocks = kc_ref.shape[0]
    attn_cols = N_Q_HEADS * HEAD_DIM
    j = pl.program_id(1)
    t0 = j * MIX_ROWS
    lt64 = _lane_lt64()
    zero = jnp.zeros((), BF16)
    ones_rows = jnp.ones((2 * SUBLANES, BLOCK), BF16)
    n_slabs = ext_ref.shape[0]
    n_groups = s_ref.shape[0]

    prev = u_ref[pl.ds(pl.multiple_of(jnp.maximum(t0 - CONV_HALO, 0), CONV_HALO), CONV_HALO), :]
    nxt = u_ref[pl.ds(pl.multiple_of(jnp.minimum(t0 + MIX_ROWS, s - CONV_HALO), CONV_HALO),
                      CONV_HALO), :]
    prev = jnp.where(j > 0, prev.astype(F32), 0.0)
    nxt = jnp.where(j < pl.num_programs(1) - 1, nxt.astype(F32), 0.0)
    main = u_ref[pl.ds(pl.multiple_of(t0, MIX_ROWS), MIX_ROWS), :].astype(F32)
    for l in range(n_slabs):
        ls = slice(l * LANES, (l + 1) * LANES)
        ext_ref[l, 0:CONV_HALO, :] = prev[:, ls]
        ext_ref[l, CONV_HALO:CONV_HALO + MIX_ROWS, :] = main[:, ls]
        ext_ref[l, CONV_HALO + MIX_ROWS:, :] = nxt[:, ls]

    chunks_per_kv = attn_cols // LANES // N_KV_HEADS
    n_stack = GQA_GROUP * BLOCK
    key_i = lax.broadcasted_iota(jnp.int32, (BLOCK, n_stack), 0)
    qry_i = lax.broadcasted_iota(jnp.int32, (BLOCK, n_stack), 1) % BLOCK
    head_of_lane = lax.broadcasted_iota(jnp.int32, (1, n_stack), 1) // BLOCK
    groups = []
    for sb in range(MIX_ROWS // BLOCK):
        qb = j * (MIX_ROWS // BLOCK) + sb
        bias_prev = jnp.where((key_i >= qry_i) & (qb > 0), 0.0, NEG_INF)
        bias_next = jnp.where((key_i <= qry_i) & (qb < n_blocks - 1), 0.0, NEG_INF)
        kb_ids = [jnp.maximum(qb - 1, 0), qb, jnp.minimum(qb + 1, n_blocks - 1)]
        biases = [None] * n_ctx_blocks + [bias_prev, None, bias_next]
        for hk in range(N_KV_HEADS):
            g = len(groups)
            cols = slice(hk * LANES, (hk + 1) * LANES)
            rows = slice(hk * HEAD_DIM, (hk + 1) * HEAD_DIM)
            q_parts = []
            sink_row = jnp.zeros((1, n_stack), F32)
            for ci in range(chunks_per_kv):
                c = hk * chunks_per_kv + ci
                q = q_ref[sb * BLOCK:(sb + 1) * BLOCK, c * LANES:(c + 1) * LANES]
                q_parts += [jnp.where(lt64, q, zero), jnp.where(lt64, zero, q)]
                for half in range(2):
                    sink_row = jnp.where(head_of_lane == 2 * ci + half,
                                         sink_ref[2 * c + half] * LOG2E, sink_row)
            keys = jnp.concatenate([kc_ref[i, :, cols] for i in range(n_ctx_blocks)]
                                   + [k_ref[kb, :, cols] for kb in kb_ids], axis=0)
            st = _dot_t(keys, jnp.concatenate(q_parts, axis=0))
            m = sink_row
            for i, bias in enumerate(biases):
                blk = st[i * BLOCK:(i + 1) * BLOCK]
                blk = blk if bias is None else blk + bias
                s_ref[g, i * BLOCK:(i + 1) * BLOCK, :] = blk
                m = jnp.maximum(m, jnp.max(blk, axis=0, keepdims=True))
            vt_ext = jnp.concatenate(
                [jnp.concatenate([vt, ones_rows], axis=0) for vt in
                 [vct_ref[i, rows, :] for i in range(n_ctx_blocks)]
                 + [vt_ref[kb, rows, :] for kb in kb_ids]], axis=1)
            groups.append((sb, hk, m, sink_row, vt_ext))

            for l in range(g, n_slabs, n_groups):
                _conv_slab(ext_ref, y_ref, cw_ref, cb_ref, l)
    cu = _layernorm_silu(y_ref, lnw_ref, lnb_ref).astype(BF16)

    for g, (sb, hk, m, sink_row, vt_ext) in enumerate(groups):
        ot = _dot(vt_ext, jnp.exp2(s_ref[g] - m).astype(BF16))
        den = ot[HEAD_DIM:HEAD_DIM + 1, :] + jnp.exp2(sink_row - m)
        ot = ot[:HEAD_DIM, :] * (1.0 / den)
        for ci in range(chunks_per_kv):
            c = hk * chunks_per_kv + ci
            pair = jnp.concatenate([ot[:, (2 * ci) * BLOCK:(2 * ci + 1) * BLOCK],
                                    ot[:, (2 * ci + 1) * BLOCK:(2 * ci + 2) * BLOCK]], axis=0)
            attn_ref[sb * BLOCK:(sb + 1) * BLOCK, c * LANES:(c + 1) * LANES] = pair.T.astype(BF16)

    mix = _dot(attn_ref[...], wout_ref[:attn_cols, :]) + _dot(cu, wout_ref[attn_cols:, :])
    o_ref[...] = x_ref[...] + mod_ref[2:3, :] * mix


def _mix_call(sink, x, mod3, q, k, vt, kc, vct, u, conv_w, conv_b, ln_w, ln_b, w_out):
    b, s, d = x.shape
    cw = u.shape[2]
    tiles = s // MIX_ROWS
    n_groups = (MIX_ROWS // BLOCK) * N_KV_HEADS
    n_keys = (kc.shape[1] + 3) * BLOCK
    const = lambda i, j: (0, 0)
    tok = lambda i, j: (i, j, 0)
    per_batch = lambda i, j: (i, 0, 0)
    blocks = lambda a: pl.BlockSpec((None,) + a.shape[1:], lambda i, j: (i, 0, 0, 0))
    return pl.pallas_call(
        _mix_kernel,
        grid=(b, tiles),
        in_specs=[
            pl.BlockSpec(memory_space=pltpu.SMEM),
            pl.BlockSpec((None, MIX_ROWS, d), tok),
            pl.BlockSpec((None, 6, d), per_batch),
            pl.BlockSpec((None, MIX_ROWS, q.shape[2]), tok),
            blocks(k),
            blocks(vt),
            blocks(kc),
            blocks(vct),
            pl.BlockSpec((None, s, cw), per_batch),
            pl.BlockSpec((CONV_KERNEL, cw), const),
            pl.BlockSpec((1, cw), const),
            pl.BlockSpec((1, cw), const),
            pl.BlockSpec((1, cw), const),
            pl.BlockSpec((q.shape[2] + cw, d), const),
        ],
        out_specs=pl.BlockSpec((None, MIX_ROWS, d), tok),
        out_shape=jax.ShapeDtypeStruct((b, s, d), F32),
        scratch_shapes=[
            pltpu.VMEM((n_groups, n_keys, GQA_GROUP * BLOCK), F32),
            pltpu.VMEM((cw // LANES, MIX_ROWS + 2 * CONV_HALO, LANES), F32),
            pltpu.VMEM((cw // LANES, MIX_ROWS, LANES), F32),
            pltpu.VMEM((MIX_ROWS, q.shape[2]), BF16),
        ],
        compiler_params=pltpu.CompilerParams(dimension_semantics=("arbitrary", "arbitrary")),
        name="mix",
    )(sink, x, mod3, q, k, vt, kc, vct, u, conv_w, conv_b, ln_w, ln_b, w_out)


def _conv3_strided(up_ref, slab, w_rows, b_rows, t):
    pieces = []
    for g in range(t // ROW_GROUP):
        base = FFN_HALO - 1 + g * ROW_GROUP
        ld = [up_ref[slab, pl.ds(base + r, SUBLANES, stride=ROW_STRIDE), :]
              for r in range(FFN_CONV + ROW_STRIDE - 1)]
        for i in range(ROW_STRIDE):
            pieces.append(ld[i] * w_rows[0] + ld[i + 1] * w_rows[1] + ld[i + 2] * w_rows[2] + b_rows)
    return jnp.concatenate(pieces, axis=0)


def _ffn_kernel(x_ref, xp_ref, xn_ref, mod_ref, nw_ref, wup_ref, cw_ref, cb_ref, wdn_ref, o_ref,
                up_ref, res_ref):
    j = pl.program_id(1)
    gain = nw_ref[...] * (1.0 + mod_ref[4:5, :])
    shift = mod_ref[3:4, :]
    x = x_ref[...]
    t = x.shape[0]
    f = wdn_ref.shape[0]
    hp = jnp.where(j > 0, _rms_modulate(xp_ref[...], gain, shift), 0.0)
    hn = jnp.where(j < pl.num_programs(1) - 1, _rms_modulate(xn_ref[...], gain, shift), 0.0)
    h = jnp.concatenate([hp, _rms_modulate(x, gain, shift), hn], axis=0).astype(BF16)

    slabs_per_chunk = FFN_CHUNK // LANES
    act_parts = []
    for ch in range(f // FFN_CHUNK):
        halves = []
        for col0 in (ch * FFN_CHUNK, f + ch * FFN_CHUNK):
            up = _dot(h, wup_ref[:, col0:col0 + FFN_CHUNK])
            outs = []
            for sl in range(slabs_per_chunk):
                slab = col0 // LANES + sl
                cs = slice(slab * LANES, (slab + 1) * LANES)
                up_ref[slab] = up[:, sl * LANES:(sl + 1) * LANES]
                w_rows = [_rows(cw_ref[k:k + 1, cs]) for k in range(FFN_CONV)]
                outs.append(_conv3_strided(up_ref, slab, w_rows, _rows(cb_ref[:, cs]), t))
            halves.append(jnp.concatenate(outs, axis=1))
        gate, val = halves
        act_parts.append((gate * _sigmoid(gate) * val).astype(BF16))
    res = _dot(jnp.concatenate(act_parts, axis=1), wdn_ref[...])
    for l in range(res_ref.shape[0]):
        for g in range(t // ROW_GROUP):
            for i in range(ROW_STRIDE):
                r0 = g * ROW_GROUP + i * SUBLANES
                res_ref[l, pl.ds(g * ROW_GROUP + i, SUBLANES, stride=ROW_STRIDE), :] = (
                    res[r0:r0 + SUBLANES, l * LANES:(l + 1) * LANES])
    res_nat = jnp.concatenate([res_ref[l] for l in range(res_ref.shape[0])], axis=1)
    o_ref[...] = x + mod_ref[5:6, :] * res_nat


def _ffn_call(x, mod3, norm_w, w_up, conv_w, conv_b, w_down):
    b, s, d = x.shape
    f2 = w_up.shape[1]
    tiles = s // FFN_ROWS
    halo_per_tile = FFN_ROWS // FFN_HALO
    n_halo = s // FFN_HALO
    const = lambda i, j: (0, 0)
    tok = lambda i, j: (i, j, 0)
    once = pl.Buffered(1)
    return pl.pallas_call(
        _ffn_kernel,
        grid=(b, tiles),
        in_specs=[
            pl.BlockSpec((None, FFN_ROWS, d), tok),
            pl.BlockSpec((None, FFN_HALO, d),
                         lambda i, j: (i, jnp.maximum(j * halo_per_tile - 1, 0), 0)),
            pl.BlockSpec((None, FFN_HALO, d),
                         lambda i, j: (i, jnp.minimum((j + 1) * halo_per_tile, n_halo - 1), 0)),
            pl.BlockSpec((None, 6, d), lambda i, j: (i, 0, 0)),
            pl.BlockSpec((1, d), const),
            pl.BlockSpec((d, f2), const, pipeline_mode=once),
            pl.BlockSpec((FFN_CONV, f2), const),
            pl.BlockSpec((1, f2), const),
            pl.BlockSpec((f2 // 2, d), const, pipeline_mode=once),
        ],
        out_specs=pl.BlockSpec((None, FFN_ROWS, d), tok),
        out_shape=jax.ShapeDtypeStruct((b, s, d), F32),
        scratch_shapes=[
            pltpu.VMEM((f2 // LANES, FFN_ROWS + 2 * FFN_HALO, LANES), F32),
            pltpu.VMEM((d // LANES, FFN_ROWS, LANES), F32),
        ],
        compiler_params=pltpu.CompilerParams(dimension_semantics=("arbitrary", "arbitrary")),
        name="ffn",
    )(x, x, x, mod3, norm_w, w_up, conv_w, conv_b, w_down)


def _rope_tables(s):
    t = jnp.arange(s)
    inv = ROPE_BASE ** (-jnp.arange(0, ROPE_AXIS_DIM, 2, dtype=F32) / ROPE_AXIS_DIM)
    ang_r = (t // GRID_W).astype(F32)[:, None] * inv
    ang_c = (t % GRID_W).astype(F32)[:, None] * inv
    ang = jnp.concatenate([ang_r, ang_r, ang_c, ang_c], axis=-1)
    half = ROPE_AXIS_DIM // 2
    sign = jnp.where((jnp.arange(HEAD_DIM) % ROPE_AXIS_DIM) < half, -1.0, 1.0).astype(F32)
    reps = LANES // HEAD_DIM
    return jnp.tile(jnp.cos(ang), (1, reps)), jnp.tile(jnp.sin(ang) * sign, (1, reps))


def _segment_ones(n):
    g = jnp.arange(n) // HEAD_DIM
    return (g[:, None] == g[None, :]).astype(BF16)


def kernel(x, c, ctx, c_ctx, w_mod, b_mod, norm_mix_w, w_in, q_norm_w, k_norm_w, sink_logit,
           conv_w, conv_b, conv_norm_w, conv_norm_b, w_out, norm_ffn_w, w_up, ffn_conv_w,
           ffn_conv_b, w_down):
    b, s, d = x.shape
    assert w_mod.shape[0] == 1, "single layer"
    q_cols = N_Q_HEADS * HEAD_DIM
    kv_cols = N_KV_HEADS * HEAD_DIM

    cc = jnp.zeros((MOD_ROWS, d), F32).at[:b].set(c).at[b].set(c_ctx)
    mod3 = _mod_call(cc, w_mod[0], b_mod).reshape(MOD_ROWS, 6, d)

    w_in_b = w_in[0].astype(BF16)
    cos, sin = _rope_tables(s)
    scale = LOG2E / math.sqrt(HEAD_DIM)
    qknw = jnp.concatenate([jnp.tile(q_norm_w[0] * scale, N_Q_HEADS),
                            jnp.tile(k_norm_w[0], N_KV_HEADS)])[None, :]
    norm_mix = norm_mix_w[0][None, :]
    seg = _segment_ones(MXU_DIM)

    kc, vct = _ctx_call(ctx, mod3, norm_mix, w_in_b[:, q_cols:q_cols + 2 * kv_cols],
                        jnp.tile(k_norm_w[0], N_KV_HEADS)[None, :], seg)
    q, k, vt, u = _proj_call(x, mod3, norm_mix, w_in_b, qknw, cos, sin, seg)
    per_block = lambda a: a.reshape(b, a.shape[1] // BLOCK, BLOCK, a.shape[2])
    x1 = _mix_call(sink_logit[0], x, mod3, q, per_block(k), vt, per_block(kc), vct, u,
                   conv_w[0], conv_b, conv_norm_w, conv_norm_b, w_out[0].astype(BF16))
    return _ffn_call(x1, mod3, norm_ffn_w[0][None, :], w_up[0].astype(BF16), ffn_conv_w[0],
                     ffn_conv_b, w_down[0].astype(BF16))
```
